```python
import math
import jax, jax.numpy as jnp
from jax import lax
import numpy as np

D_MODEL = 2048
BATCH = 4
SEQ = 4096
DEPTH = 1
DEC_BATCH = 128
DEC_SEQ = 4
PAST_LEN = 16384
PAGE_SIZE = 128

RMS_EPS = 1e-6
MIX_WIDTH = D_MODEL
HGRN_EXPAND = 128
HGRN_VAL_WIDTH = MIX_WIDTH // 2
HGRN_HEADS = HGRN_VAL_WIDTH // HGRN_EXPAND
HGRN_DK = HGRN_EXPAND
HGRN_DV = HGRN_VAL_WIDTH // HGRN_HEADS
HGRN_KEY_WIDTH = HGRN_HEADS * HGRN_DK
HGRN_CHUNK = 16
SWA_WIDTH = MIX_WIDTH - HGRN_VAL_WIDTH
SWA_HEAD_DIM = 64
SWA_HEADS = SWA_WIDTH // SWA_HEAD_DIM
SWA_KV_HEADS = 2
SWA_GROUP = SWA_HEADS // SWA_KV_HEADS
SWA_KV_WIDTH = SWA_KV_HEADS * SWA_HEAD_DIM
SWA_WINDOW = 128
REL_BUCKETS = 32
REL_MAX_EXACT = REL_BUCKETS // 2
REL_MAX_DIST = 128
N_MEM = 256
MEM_HEADS = 4
MEM_HEAD_DIM = 128
MEM_WIDTH = MEM_HEADS * MEM_HEAD_DIM
FFN_HIDDEN = -(-8 * D_MODEL // (3 * 256)) * 256
IN_SIZES = (HGRN_KEY_WIDTH, HGRN_KEY_WIDTH, HGRN_VAL_WIDTH, HGRN_VAL_WIDTH,
            SWA_WIDTH, SWA_KV_WIDTH, SWA_KV_WIDTH)
IN_WIDTH = sum(IN_SIZES)
IN_SPLITS = tuple(sum(IN_SIZES[:i + 1]) for i in range(len(IN_SIZES) - 1))

kernel_name = "hymba_hgrn2_swa_sink_memxattn_step"


def rmsnorm(x, w, eps=RMS_EPS):
    xf = x.astype(jnp.float32)
    xf = xf * lax.rsqrt(jnp.mean(xf * xf, axis=-1, keepdims=True) + eps)
    return (xf * w.astype(jnp.float32)).astype(x.dtype)


def hgrn_lower_bounds(lb_logits):
    p = jax.nn.softmax(lb_logits.astype(jnp.float32), axis=0)
    return jnp.cumsum(p, axis=0)[:DEPTH]


def hgrn2_scan(q, k, g, v, s0):
    B, L, H, DK = q.shape
    C = math.gcd(L, HGRN_CHUNK)
    N = L // C
    qc, kc, gc, vc = (t.reshape(B, N, C, H, t.shape[-1]) for t in (q, k, g, v))
    G = jnp.cumsum(gc, axis=2)
    causal = jnp.tril(jnp.ones((C, C), bool))[None, None, :, :, None, None]
    diff = G[:, :, :, None] - G[:, :, None, :]
    decay = jnp.exp(jnp.where(causal, diff, -jnp.inf))
    attn = jnp.einsum('bnthc,bnshc,bntshc->bnhts', qc, kc, decay)
    o_intra = jnp.einsum('bnhts,bnshe->bnthe', attn, vc)
    g_last = G[:, :, -1]
    q_in = qc * jnp.exp(G)
    k_up = kc * jnp.exp(g_last[:, :, None] - G)

    def step(S, xs):
        q_n, k_n, v_n, gl_n = xs
        o = jnp.einsum('bthc,bhce->bthe', q_n, S)
        S = jnp.exp(gl_n)[..., None] * S + jnp.einsum('bshc,bshe->bhce', k_n, v_n)
        return S, o

    xs = tuple(jnp.moveaxis(t, 1, 0) for t in (q_in, k_up, vc, g_last))
    s_final, o_inter = lax.scan(step, s0.astype(jnp.float32), xs)
    o = o_intra + jnp.moveaxis(o_inter, 0, 1)
    return o.reshape(B, L, H, v.shape[-1]), s_final


def hgrn_group(hq, hf, hi, hg, lb, onorm_w, s0):
    B, L, _ = hq.shape
    f = lb + (1.0 - lb) * jax.nn.sigmoid(hf.astype(jnp.float32))
    heads = lambda t, d: t.astype(jnp.float32).reshape(B, L, HGRN_HEADS, d)
    q = heads(hq, HGRN_DK)
    g = heads(jnp.log(f), HGRN_DK)
    k = heads(1.0 - f, HGRN_DK)
    i = heads(hi, HGRN_DV)
    o, s_new = hgrn2_scan(q, k, g, i, s0)
    o = rmsnorm(o, onorm_w) * jax.nn.silu(heads(hg, HGRN_DV))
    return o.reshape(B, L, HGRN_VAL_WIDTH).astype(hq.dtype), s_new


def rel_bucket(dist):
    n = jnp.maximum(dist, 0)
    nf = jnp.maximum(n, 1).astype(jnp.float32)
    large = REL_MAX_EXACT + (jnp.log(nf / REL_MAX_EXACT) / math.log(REL_MAX_DIST / REL_MAX_EXACT)
                             * (REL_BUCKETS - REL_MAX_EXACT)).astype(jnp.int32)
    large = jnp.minimum(large, REL_BUCKETS - 1)
    return jnp.where(n < REL_MAX_EXACT, n, large)


def swa_band(q, k, v, key_valid, sinks, rel_bias):
    B, N, Qn, _ = q.shape
    Kn = k.shape[2]
    qh = q.astype(jnp.float32).reshape(B, N, Qn, SWA_KV_HEADS, SWA_GROUP, SWA_HEAD_DIM)
    kh = k.astype(jnp.float32).reshape(B, N, Kn, SWA_KV_HEADS, SWA_HEAD_DIM)
    vh = v.astype(jnp.float32).reshape(B, N, Kn, SWA_KV_HEADS, SWA_HEAD_DIM)
    dist = (jnp.arange(Qn)[:, None] + (Kn - Qn)) - jnp.arange(Kn)[None, :]
    bias = rel_bias.astype(jnp.float32)[rel_bucket(dist)]
    bias = jnp.transpose(bias, (2, 0, 1)).reshape(SWA_KV_HEADS, SWA_GROUP, Qn, Kn)
    s = jnp.einsum('bnqkgd,bnskd->bnkgqs', qh, kh) * (SWA_HEAD_DIM ** -0.5) + bias
    mask = ((dist >= 0) & (dist < SWA_WINDOW))[None] & key_valid[:, None, :]
    s = jnp.where(mask[None, :, None, None], s, -jnp.inf)
    sink = sinks.astype(jnp.float32).reshape(SWA_KV_HEADS, SWA_GROUP)[:, :, None, None]
    m = jnp.maximum(jnp.max(s, axis=-1, keepdims=True), sink)
    p = jnp.exp(s - m)
    p = p / (jnp.sum(p, axis=-1, keepdims=True) + jnp.exp(sink - m))
    o = jnp.einsum('bnkgqs,bnskd->bnqkgd', p, vh)
    return o.reshape(B, N, Qn, SWA_WIDTH).astype(q.dtype)


def token_mixer(h, s0, k_buf, v_buf, lb, w_in, hgrn_onorm_w, swa_sinks, rel_bias, w_out):
    B, L, _ = h.shape
    hq, hf, hi, hg, sq, sk, sv = jnp.split(h @ w_in, IN_SPLITS, axis=-1)
    a_out, s_new = hgrn_group(hq, hf, hi, hg, lb, hgrn_onorm_w, s0)
    if k_buf is None:
        nb = L // SWA_WINDOW
        qb = sq.reshape(B, nb, SWA_WINDOW, SWA_WIDTH)

        def band(t):
            tb = t.reshape(B, nb, SWA_WINDOW, SWA_KV_WIDTH)
            prev = jnp.concatenate([jnp.zeros_like(tb[:, :1]), tb[:, :-1]], axis=1)
            return jnp.concatenate([prev, tb], axis=2)

        kb, vb = band(sk), band(sv)
        key_valid = (jnp.arange(nb)[:, None] > 0) | (jnp.arange(2 * SWA_WINDOW)[None, :] >= SWA_WINDOW)
        k_last, v_last = sk[:, -SWA_WINDOW:], sv[:, -SWA_WINDOW:]
    else:
        qb = sq[:, None]
        k_all = jnp.concatenate([k_buf.reshape(B, SWA_WINDOW, SWA_KV_WIDTH), sk], axis=1)
        v_all = jnp.concatenate([v_buf.reshape(B, SWA_WINDOW, SWA_KV_WIDTH), sv], axis=1)
        kb, vb = k_all[:, None], v_all[:, None]
        key_valid = jnp.ones((1, SWA_WINDOW + L), bool)
        k_last, v_last = k_all[:, -SWA_WINDOW:], v_all[:, -SWA_WINDOW:]
    b_out = swa_band(qb, kb, vb, key_valid, swa_sinks, rel_bias).reshape(B, L, SWA_WIDTH)
    y = jnp.concatenate([a_out, b_out], axis=-1) @ w_out
    kv_shape = (B, SWA_WINDOW, SWA_KV_HEADS, SWA_HEAD_DIM)
    return y, s_new, k_last.reshape(kv_shape), v_last.reshape(kv_shape)


def memory_kv(mem, mem_norm_w, w_mk, w_mv):
    B = mem.shape[0]
    m = rmsnorm(mem, mem_norm_w)
    k = (m @ w_mk).reshape(B, N_MEM, MEM_HEADS, MEM_HEAD_DIM)
    v = (m @ w_mv).reshape(B, N_MEM, MEM_HEADS, MEM_HEAD_DIM)
    return k, v


def memory_attend(h, mem_k, mem_v, w_mq, w_mo):
    B, L, _ = h.shape
    q = (h @ w_mq).reshape(B, L, MEM_HEADS, MEM_HEAD_DIM).astype(jnp.float32)
    s = jnp.einsum('blhd,bmhd->bhlm', q, mem_k.astype(jnp.float32)) * (MEM_HEAD_DIM ** -0.5)
    p = jax.nn.softmax(s, axis=-1)
    o = jnp.einsum('bhlm,bmhd->blhd', p, mem_v.astype(jnp.float32))
    return o.reshape(B, L, MEM_WIDTH).astype(h.dtype) @ w_mo


def swiglu(h, w_gate, w_up, w_down):
    return (jax.nn.silu(h @ w_gate) * (h @ w_up)) @ w_down


def decoder_layer(x, s0, k_buf, v_buf, mem_k, mem_v, lb, norm_mix_w, w_in, hgrn_onorm_w,
                  swa_sinks, rel_bias, w_out, norm_xattn_w, w_mq, w_mo, norm_ffn_w,
                  w_gate, w_up, w_down):
    y, s_new, k_new, v_new = token_mixer(rmsnorm(x, norm_mix_w), s0, k_buf, v_buf, lb, w_in,
                                         hgrn_onorm_w, swa_sinks, rel_bias, w_out)
    x = x + y
    x = x + memory_attend(rmsnorm(x, norm_xattn_w), mem_k, mem_v, w_mq, w_mo)
    x = x + swiglu(rmsnorm(x, norm_ffn_w), w_gate, w_up, w_down)
    return x, s_new, k_new, v_new


def setup_inputs(seed: int = 0) -> dict:
    key = jax.random.key(seed)
    ks = jax.random.split(key, 32)
    nrm = lambda k, shape, scale=1.0: scale * jax.random.normal(k, shape, jnp.float32)
    gain = lambda k, shape: 1.0 + 0.05 * jax.random.normal(k, shape, jnp.float32)
    kv_shape = (DEPTH, DEC_BATCH, SWA_WINDOW, SWA_KV_HEADS, SWA_HEAD_DIM)
    mem_shape = (DEPTH, DEC_BATCH, N_MEM, MEM_HEADS, MEM_HEAD_DIM)
    return {
        "x_prompt": nrm(ks[0], (BATCH, SEQ, D_MODEL)),
        "x_sample": nrm(ks[1], (DEC_BATCH, DEC_SEQ, D_MODEL)),
        "state_hgrn": nrm(ks[2], (DEPTH, DEC_BATCH, HGRN_HEADS, HGRN_DK, HGRN_DV), 0.5),
        "cache_swa_k": nrm(ks[3], kv_shape),
        "cache_swa_v": nrm(ks[4], kv_shape),
        "cache_mem_k": nrm(ks[5], mem_shape),
        "cache_mem_v": nrm(ks[6], mem_shape),
        "mem_prompt": nrm(ks[7], (BATCH, N_MEM, D_MODEL)),
        "norm_mix_w": gain(ks[8], (DEPTH, D_MODEL)),
        "w_in": nrm(ks[9], (DEPTH, D_MODEL, IN_WIDTH), D_MODEL ** -0.5),
        "hgrn_lb_logits": nrm(ks[10], (DEPTH + 1, HGRN_KEY_WIDTH), 0.5),
        "hgrn_onorm_w": gain(ks[11], (DEPTH, HGRN_DV)),
        "swa_sinks": nrm(ks[12], (DEPTH, SWA_HEADS), 0.5),
        "rel_bias": nrm(ks[13], (REL_BUCKETS, SWA_HEADS), 0.2),
        "w_out": nrm(ks[14], (DEPTH, MIX_WIDTH, D_MODEL), MIX_WIDTH ** -0.5),
        "norm_xattn_w": gain(ks[15], (DEPTH, D_MODEL)),
        "mem_norm_w": gain(ks[16], (DEPTH, D_MODEL)),
        "w_mq": nrm(ks[17], (DEPTH, D_MODEL, MEM_WIDTH), D_MODEL ** -0.5),
        "w_mk": nrm(ks[18], (DEPTH, D_MODEL, MEM_WIDTH), D_MODEL ** -0.5),
        "w_mv": nrm(ks[19], (DEPTH, D_MODEL, MEM_WIDTH), D_MODEL ** -0.5),
        "w_mo": nrm(ks[20], (DEPTH, MEM_WIDTH, D_MODEL), MEM_WIDTH ** -0.5),
        "norm_ffn_w": gain(ks[21], (DEPTH, D_MODEL)),
        "w_gate": nrm(ks[22], (DEPTH, D_MODEL, FFN_HIDDEN), D_MODEL ** -0.5),
        "w_up": nrm(ks[23], (DEPTH, D_MODEL, FFN_HIDDEN), D_MODEL ** -0.5),
        "w_down": nrm(ks[24], (DEPTH, FFN_HIDDEN, D_MODEL), FFN_HIDDEN ** -0.5),
        "norm_final_w": gain(ks[25], (D_MODEL,)),
    }


def reference(x_prompt, x_sample, state_hgrn, cache_swa_k, cache_swa_v, cache_mem_k, cache_mem_v,
              mem_prompt, norm_mix_w, w_in, hgrn_lb_logits, hgrn_onorm_w, swa_sinks, rel_bias,
              w_out, norm_xattn_w, mem_norm_w, w_mq, w_mk, w_mv, w_mo, norm_ffn_w, w_gate, w_up,
              w_down, norm_final_w):
    lbs = hgrn_lower_bounds(hgrn_lb_logits)
    xp, xs = x_prompt, x_sample
    p_s, p_k, p_v, p_mk, p_mv, s_s, s_k, s_v = [], [], [], [], [], [], [], []
    for l in range(DEPTH):
        shared = (lbs[l], norm_mix_w[l], w_in[l], hgrn_onorm_w[l], swa_sinks[l], rel_bias,
                  w_out[l], norm_xattn_w[l], w_mq[l], w_mo[l], norm_ffn_w[l], w_gate[l],
                  w_up[l], w_down[l])
        mk, mv = memory_kv(mem_prompt, mem_norm_w[l], w_mk[l], w_mv[l])
        s0 = jnp.zeros((BATCH, HGRN_HEADS, HGRN_DK, HGRN_DV), jnp.float32)
        xp, sp, kp, vp = decoder_layer(xp, s0, None, None, mk, mv, *shared)
        xs, ss, ksn, vsn = decoder_layer(xs, state_hgrn[l], cache_swa_k[l], cache_swa_v[l],
                                         cache_mem_k[l], cache_mem_v[l], *shared)
        p_s.append(sp); p_k.append(kp); p_v.append(vp); p_mk.append(mk); p_mv.append(mv)
        s_s.append(ss); s_k.append(ksn); s_v.append(vsn)
    y_prompt = rmsnorm(xp, norm_final_w)
    y_sample = rmsnorm(xs, norm_final_w)
    return (y_prompt, y_sample,
            jnp.stack(p_s).astype(x_prompt.dtype), jnp.stack(p_k), jnp.stack(p_v),
            jnp.stack(p_mk), jnp.stack(p_mv),
            jnp.stack(s_s).astype(state_hgrn.dtype), jnp.stack(s_k), jnp.stack(s_v))
```

```python
import functools
import math

import jax
import jax.numpy as jnp
from jax import lax
from jax.experimental import pallas as pl
from jax.experimental.pallas import tpu as pltpu

RMS_EPS = 1e-6
SWA_HEAD_DIM = 64
REL_BUCKETS = 32
REL_MAX_EXACT = REL_BUCKETS // 2
REL_MAX_DIST = 128
MEM_HEAD_DIM = 128
HGRN_SUB = 16
HGRN_CHUNK = 128
EXP_CLAMP = 80.0
MASKED = -1e30
LANE = 128
SUBLANE = 8
ROW_CHUNK = 256
VMEM_LIMIT = 56 * 1024 * 1024

BF16 = jnp.bfloat16
F32 = jnp.float32


def _cparams(n_axes):
    return pltpu.CompilerParams(dimension_semantics=("arbitrary",) * n_axes,
                                vmem_limit_bytes=VMEM_LIMIT)


def _dot(a, b):
    return jnp.dot(a, b, preferred_element_type=F32)


def _dot_nt(a, b):
    return lax.dot_general(a, b, (((1,), (1,)), ((), ())), preferred_element_type=F32)


def _dot_tn(a, b):
    return lax.dot_general(a, b, (((0,), (0,)), ((), ())), preferred_element_type=F32)


def _rms(x, gain):
    ms = jnp.mean(x * x, axis=-1, keepdims=True)
    return x * lax.rsqrt(ms + RMS_EPS) * gain


def _split3(x):
    hi = x.astype(BF16)
    r1 = x - hi.astype(F32)
    mid = r1.astype(BF16)
    lo = (r1 - mid.astype(F32)).astype(BF16)
    return hi, mid, lo


def _for_row_chunks(n_rows, fn, rows=ROW_CHUNK):
    rows = min(rows, n_rows)
    assert n_rows % rows == 0

    def body(i, carry):
        fn(pl.ds(pl.multiple_of(i * rows, rows), rows))
        return carry

    lax.fori_loop(0, n_rows // rows, body, 0)


def _norm_matmul_kernel(x_ref, g_ref, w_ref, o_ref, h_ref):
    @pl.when(pl.program_id(1) == 0)
    def _():
        def norm(r):
            h_ref[r, :] = _rms(x_ref[r, :], g_ref[...]).astype(BF16)
        _for_row_chunks(x_ref.shape[0], norm)

    o_ref[...] = _dot(h_ref[...], w_ref[...]).astype(o_ref.dtype)


def _norm_matmul(x, gain, w, tm, tn, name):
    m, k = x.shape
    n = w.shape[1]
    tm = min(tm, m)
    tn = min(tn, n)
    assert m % tm == 0 and n % tn == 0
    return pl.pallas_call(
        _norm_matmul_kernel,
        out_shape=jax.ShapeDtypeStruct((m, n), F32),
        grid=(m // tm, n // tn),
        in_specs=[pl.BlockSpec((tm, k), lambda i, j: (i, 0)),
                  pl.BlockSpec((1, k), lambda i, j: (0, 0)),
                  pl.BlockSpec((k, tn), lambda i, j: (0, j))],
        out_specs=pl.BlockSpec((tm, tn), lambda i, j: (i, j)),
        scratch_shapes=[pltpu.VMEM((tm, k), BF16)],
        compiler_params=_cparams(2),
        name=name,
    )(x, gain, w)


def _lower_bound(lb_logits, layer):
    m = jnp.max(lb_logits, axis=0, keepdims=True)
    e = jnp.exp(lb_logits - m)
    return jnp.sum(e[:layer + 1], axis=0, keepdims=True) / jnp.sum(e, axis=0, keepdims=True)


def _gates(hf, lb):
    f = lb + (1.0 - lb) * jax.nn.sigmoid(hf)
    return jnp.log(f), 1.0 - f


def _head_norm_gate(o, onw, hg):
    return _rms(o, onw) * (hg * jax.nn.sigmoid(hg))


def _hgrn_prompt_kernel(q_ref, f_ref, i_ref, og_ref, lbl_ref, onw_ref,
                        o_ref, s_ref, st_ref, k_s, g_s, inter_s, intra_s, *, heads, dk, dv):
    c = q_ref.shape[0]
    nsub = c // HGRN_SUB
    n = pl.program_id(1)

    @pl.when(n == 0)
    def _():
        st_ref[...] = jnp.zeros_like(st_ref)

    lb = _lower_bound(lbl_ref[...], 0)
    g, kk = _gates(f_ref[...], lb)
    row = lax.broadcasted_iota(jnp.int32, (c, c), 0)
    col = lax.broadcasted_iota(jnp.int32, (c, c), 1)
    tri = (col <= row).astype(BF16)
    g_hi, g_mid, g_lo = _split3(g)
    gc = _dot(tri, g_hi) + _dot(tri, g_mid) + _dot(tri, g_lo)
    k_s[...] = kk
    g_s[...] = gc

    starts = [jnp.zeros((1, gc.shape[1]), F32)] + [
        gc[HGRN_SUB * i - 1:HGRN_SUB * i, :] for i in range(1, nsub)]
    ends = [gc[HGRN_SUB * (i + 1) - 1:HGRN_SUB * (i + 1), :] for i in range(nsub)]
    worst = jnp.max(jnp.concatenate([s - e for s, e in zip(starts, ends)], axis=0))

    sub_row = lax.broadcasted_iota(jnp.int32, (HGRN_SUB, c), 0)
    sub_col = lax.broadcasted_iota(jnp.int32, (HGRN_SUB, c), 1)

    for h in range(heads):
        ks = slice(h * dk, (h + 1) * dk)
        vs = slice(h * dv, (h + 1) * dv)
        gh = gc[:, ks]
        qh = q_ref[:, ks]
        kh = kk[:, ks]
        vb = i_ref[:, vs].astype(BF16)
        g_last = gh[c - 1:c, :]
        st = st_ref[h]
        q_in = (qh * jnp.exp(gh)).astype(BF16)
        inter_s[:, vs] = _dot_nt(q_in, st.astype(BF16))
        k_up = (kh * jnp.exp(g_last - gh)).astype(BF16)
        st_ref[h] = st * jnp.exp(g_last) + _dot_tn(vb, k_up)
        blocks = []
        for i in range(nsub):
            r0 = HGRN_SUB * i
            nk = r0 + HGRN_SUB
            gs = starts[i][:, ks]
            qt = (qh[r0:nk] * jnp.exp(gh[r0:nk] - gs)).astype(BF16)
            kt = (kh[:nk] * jnp.exp(jnp.minimum(gs - gh[:nk], EXP_CLAMP))).astype(BF16)
            if nk < c:
                kt = jnp.concatenate([kt, jnp.zeros((c - nk, dk), BF16)], axis=0)
            a = _dot_nt(qt, kt)
            blocks.append(jnp.where(sub_col <= sub_row + r0, a, 0.0).astype(BF16))
        intra_s[:, vs] = _dot(jnp.concatenate(blocks, axis=0), vb)

    @pl.when(worst > EXP_CLAMP)
    def _():
        rows8 = lax.broadcasted_iota(jnp.int32, (8, 1), 0)
        key_row = lax.broadcasted_iota(jnp.int32, (c, 1), 0)

        def body(t8, carry):
            base = pl.multiple_of(t8 * 8, 8)
            acc = [jnp.zeros((8, dv), F32) for _ in range(heads)]
            for r in range(8):
                t = base + r
                g_t = g_s[pl.ds(t, 1), :]
                g_p = jnp.where(t > 0, g_s[pl.ds(jnp.maximum(t - 1, 0), 1), :], 0.0)
                q_t = q_ref[pl.ds(t, 1), :]
                k_t = k_s[pl.ds(t, 1), :]
                r_t = q_t * jnp.exp(g_t - g_p)
                w = k_s[...] * jnp.exp(jnp.minimum(g_p - g_s[...], 0.0)) * r_t
                d_t = q_t * k_t
                for h in range(heads):
                    ks = slice(h * dk, (h + 1) * dk)
                    vs = slice(h * dv, (h + 1) * dv)
                    a_col = jnp.sum(w[:, ks], axis=1, keepdims=True)
                    a_dg = jnp.sum(d_t[:, ks], axis=1, keepdims=True)
                    a_col = jnp.where(key_row < t, a_col, 0.0) + jnp.where(key_row == t, a_dg, 0.0)
                    o_row = jnp.sum(a_col * i_ref[:, vs], axis=0, keepdims=True)
                    acc[h] = acc[h] + jnp.where(rows8 == r, o_row, 0.0)
            for h in range(heads):
                intra_s[pl.ds(base, 8), h * dv:(h + 1) * dv] = acc[h]
            return carry

        lax.fori_loop(0, c // 8, body, 0)

    onw = onw_ref[...]
    for h in range(heads):
        vs = slice(h * dv, (h + 1) * dv)
        o = inter_s[:, vs] + intra_s[:, vs]
        o_ref[:, vs] = _head_norm_gate(o, onw, og_ref[:, vs]).astype(o_ref.dtype)

    @pl.when(n == pl.num_programs(1) - 1)
    def _():
        for h in range(heads):
            s_ref[0, h] = st_ref[h].T


def _hgrn_prompt(proj, lb_logits, onorm_w, batch, seq, heads, dk, dv):
    width = heads * dk
    c = HGRN_CHUNK
    assert seq % c == 0 and heads * dv == width
    nc = seq // c
    blk = lambda j: pl.BlockSpec((c, width), lambda b, n, j=j: (b * nc + n, j))
    return pl.pallas_call(
        functools.partial(_hgrn_prompt_kernel, heads=heads, dk=dk, dv=dv),
        out_shape=(jax.ShapeDtypeStruct((batch * seq, width), BF16),
                   jax.ShapeDtypeStruct((batch, heads, dk, dv), F32)),
        grid=(batch, nc),
        in_specs=[blk(0), blk(1), blk(2), blk(3),
                  pl.BlockSpec(lb_logits.shape, lambda b, n: (0, 0)),
                  pl.BlockSpec((1, dv), lambda b, n: (0, 0))],
        out_specs=(pl.BlockSpec((c, width), lambda b, n: (b * nc + n, 0)),
                   pl.BlockSpec((1, heads, dk, dv), lambda b, n: (b, 0, 0, 0))),
        scratch_shapes=[pltpu.VMEM((heads, dv, dk), F32),
                        pltpu.VMEM((c, width), F32), pltpu.VMEM((c, width), F32),
                        pltpu.VMEM((c, width), F32), pltpu.VMEM((c, width), F32)],
        compiler_params=_cparams(2),
        name="hgrn_prompt",
    )(proj, proj, proj, proj, lb_logits, onorm_w)


def _hgrn_sample_kernel(q_ref, f_ref, i_ref, og_ref, s0_ref, lbl_ref, onw_ref,
                        o_ref, s_ref, *, heads, dk, dv):
    bb, rows, width = q_ref.shape
    lb = _lower_bound(lbl_ref[...], 0)
    onw = onw_ref[...]
    ones8 = jnp.ones((SUBLANE, dv), BF16)
    zrows = jnp.zeros((SUBLANE - rows, width), F32)
    pad8 = lambda x: jnp.concatenate([x, zrows], axis=0)
    row_id = lax.broadcasted_iota(jnp.int32, (SUBLANE, 1), 0)
    for b in range(bb):
        q, v, og = pad8(q_ref[b]), pad8(i_ref[b]), pad8(og_ref[b])
        g, kk = _gates(pad8(f_ref[b]), lb)
        gcs = [g[0:1]]
        for t in range(1, rows):
            gcs.append(gcs[-1] + g[t:t + 1])
        g_last = gcs[-1]
        gc = jnp.concatenate(gcs + [g_last] * (SUBLANE - rows), axis=0)
        q_in = q * jnp.exp(gc)
        k_up = (kk * jnp.exp(g_last - gc)).astype(BF16)
        vb = v.astype(BF16)
        prods = [q * kk[s:s + 1] * jnp.exp(jnp.minimum(gc - gcs[s], 0.0)) for s in range(rows)]
        gl_hi, gl_mid, gl_lo = _split3(g_last)
        dec_rows = jnp.concatenate([gl_hi.astype(F32), gl_mid.astype(F32), gl_lo.astype(F32),
                                    jnp.zeros((SUBLANE - 3, width), F32)], axis=0).astype(BF16)
        for h in range(heads):
            ks = slice(h * dk, (h + 1) * dk)
            vs = slice(h * dv, (h + 1) * dv)
            s0 = s0_ref[b, h]
            o = _dot(q_in[:, ks].astype(BF16), s0.astype(BF16))
            for s in range(rows):
                a = jnp.sum(prods[s][:, ks], axis=1, keepdims=True)
                o = o + jnp.where(row_id >= s, a, 0.0) * v[s:s + 1, vs]
            o = _head_norm_gate(o, onw, og[:, vs])
            o_ref[b, :, vs] = o[:rows].astype(o_ref.dtype)
            decay = jnp.exp(_dot_tn(dec_rows[:, ks], ones8))
            s_ref[b, h] = s0 * decay + _dot_tn(k_up[:, ks], vb[:, vs])


def _hgrn_sample(proj, state, lb_logits, onorm_w, batch, rows, heads, dk, dv, bb):
    width = heads * dk
    assert batch % bb == 0 and rows <= 8
    proj3 = proj.reshape(batch, rows, proj.shape[-1])
    blk = lambda j: pl.BlockSpec((bb, rows, width), lambda b, j=j: (b, 0, j))
    return pl.pallas_call(
        functools.partial(_hgrn_sample_kernel, heads=heads, dk=dk, dv=dv),
        out_shape=(jax.ShapeDtypeStruct((batch, rows, width), BF16),
                   jax.ShapeDtypeStruct((batch, heads, dk, dv), F32)),
        grid=(batch // bb,),
        in_specs=[blk(0), blk(1), blk(2), blk(3),
                  pl.BlockSpec((bb, heads, dk, dv), lambda b: (b, 0, 0, 0)),
                  pl.BlockSpec(lb_logits.shape, lambda b: (0, 0)),
                  pl.BlockSpec((1, dv), lambda b: (0, 0))],
        out_specs=(pl.BlockSpec((bb, rows, width), lambda b: (b, 0, 0)),
                   pl.BlockSpec((bb, heads, dk, dv), lambda b: (b, 0, 0, 0))),
        compiler_params=_cparams(1),
        name="hgrn_sample",
    )(proj3, proj3, proj3, proj3, state, lb_logits, onorm_w)


def _rel_bucket_table(nq, nk):
    dist = (jnp.arange(nq)[:, None] + (nk - nq)) - jnp.arange(nk)[None, :]
    n = jnp.maximum(dist, 0)
    nf = jnp.maximum(n, 1).astype(F32)
    large = REL_MAX_EXACT + (jnp.log(nf / REL_MAX_EXACT) / math.log(REL_MAX_DIST / REL_MAX_EXACT)
                             * (REL_BUCKETS - REL_MAX_EXACT)).astype(jnp.int32)
    large = jnp.minimum(large, REL_BUCKETS - 1)
    bucket = jnp.where(n < REL_MAX_EXACT, n, large)
    return jnp.where((dist >= 0) & (dist < REL_MAX_DIST), bucket, -1).astype(jnp.int32)


def _build_bias(bias_s, bucket_ref, rel_ref, heads):
    bucket = bucket_ref[...]
    for h in range(heads):
        acc = jnp.full(bucket.shape, MASKED, F32)
        for bkt in range(REL_BUCKETS):
            acc = jnp.where(bucket == bkt, rel_ref[bkt, h], acc)
        bias_s[h] = acc


def _swa_core(q, k_all, v_all, bias_s, sink_ref, extra_mask, kv_heads, group):
    assert kv_heads == 2 and 2 * SWA_HEAD_DIM == LANE and group % 2 == 0
    nq = q.shape[0]
    half = group // 2
    lane = lax.broadcasted_iota(jnp.int32, (1, LANE), 1)
    low = lane < SWA_HEAD_DIM
    scale = SWA_HEAD_DIM ** -0.5
    k_sw = pltpu.roll(k_all, SWA_HEAD_DIM, axis=1)
    v_sw = pltpu.roll(v_all, SWA_HEAD_DIM, axis=1)
    zero = jnp.zeros_like(k_all)
    qblk = [q[:, j * LANE:(j + 1) * LANE] for j in range(group)]
    res = []
    for kind, (kk_, vv_) in enumerate(((k_all, v_all), (k_sw, v_sw))):
        if kind == 0:
            q2 = [jnp.where(low, qblk[j], qblk[half + j]) for j in range(half)]
            heads_lo = [2 * j for j in range(half)]
            heads_hi = [group + 2 * j + 1 for j in range(half)]
        else:
            q2 = [jnp.where(low, qblk[half + j], qblk[j]) for j in range(half)]
            heads_lo = [group + 2 * j for j in range(half)]
            heads_hi = [2 * j + 1 for j in range(half)]
        q2 = jnp.concatenate(q2, axis=0).astype(BF16)
        acc = None
        for hds, keep_low in ((heads_lo, True), (heads_hi, False)):
            kb = (jnp.where(low, kk_, zero) if keep_low else jnp.where(low, zero, kk_)).astype(BF16)
            vb = (jnp.where(low, vv_, zero) if keep_low else jnp.where(low, zero, vv_)).astype(BF16)
            bias = jnp.concatenate([bias_s[hd] for hd in hds], axis=0)
            sink = jnp.concatenate([jnp.full((nq, 1), sink_ref[0, hd], F32) for hd in hds], axis=0)
            s = _dot_nt(q2, kb) * scale + bias
            if extra_mask is not None:
                s = jnp.where(extra_mask, s, MASKED)
            m = jnp.maximum(jnp.max(s, axis=-1, keepdims=True), sink)
            p = jnp.exp(s - m)
            den = jnp.sum(p, axis=-1, keepdims=True) + jnp.exp(sink - m)
            o = _dot(p.astype(BF16), vb) / den
            acc = o if acc is None else acc + o
        res.append(acc)
    out = [None] * group
    for j in range(half):
        r0, r1 = res[0][j * nq:(j + 1) * nq], res[1][j * nq:(j + 1) * nq]
        out[j] = jnp.where(low, r0, r1)
        out[half + j] = jnp.where(low, r1, r0)
    return out


def _swa_prompt_kernel(q_ref, kc_ref, kp_ref, vc_ref, vp_ref, bucket_ref, rel_ref, sink_ref,
                       o_ref, bias_s, *, heads, kv_heads):
    b, n = pl.program_id(0), pl.program_id(1)

    @pl.when((b == 0) & (n == 0))
    def _():
        _build_bias(bias_s, bucket_ref, rel_ref, heads)

    nq = q_ref.shape[0]
    k_all = jnp.concatenate([kp_ref[...], kc_ref[...]], axis=0)
    v_all = jnp.concatenate([vp_ref[...], vc_ref[...]], axis=0)
    key_id = lax.broadcasted_iota(jnp.int32, (1, k_all.shape[0]), 1)
    valid = (key_id >= nq) | (n > 0)
    out = _swa_core(q_ref[...], k_all, v_all, bias_s, sink_ref, valid, kv_heads, heads // kv_heads)
    for j, o in enumerate(out):
        o_ref[:, j * LANE:(j + 1) * LANE] = o.astype(o_ref.dtype)


def _swa_prompt(proj, rel_bias, sinks, batch, seq, window, heads, kv_heads):
    qw = heads * SWA_HEAD_DIM
    kw = kv_heads * SWA_HEAD_DIM
    assert kw == LANE and seq % window == 0 and qw % kw == 0
    nb = seq // window
    kcol = qw // kw
    bucket = _rel_bucket_table(window, 2 * window)
    cur = lambda b, n: b * nb + n
    prev = lambda b, n: b * nb + jnp.maximum(n - 1, 0)
    smem = pl.BlockSpec(memory_space=pltpu.SMEM)
    return pl.pallas_call(
        functools.partial(_swa_prompt_kernel, heads=heads, kv_heads=kv_heads),
        out_shape=jax.ShapeDtypeStruct((batch * seq, qw), BF16),
        grid=(batch, nb),
        in_specs=[pl.BlockSpec((window, qw), lambda b, n: (cur(b, n), 0)),
                  pl.BlockSpec((window, kw), lambda b, n: (cur(b, n), kcol)),
                  pl.BlockSpec((window, kw), lambda b, n: (prev(b, n), kcol)),
                  pl.BlockSpec((window, kw), lambda b, n: (cur(b, n), kcol + 1)),
                  pl.BlockSpec((window, kw), lambda b, n: (prev(b, n), kcol + 1)),
                  pl.BlockSpec(bucket.shape, lambda b, n: (0, 0)),
                  smem, smem],
        out_specs=pl.BlockSpec((window, qw), lambda b, n: (cur(b, n), 0)),
        scratch_shapes=[pltpu.VMEM((heads, window, 2 * window), F32)],
        compiler_params=_cparams(2),
        name="swa_prompt",
    )(proj, proj, proj, proj, proj, bucket, rel_bias, sinks)


def _swa_sample_kernel(q_ref, kn_ref, vn_ref, kc_ref, vc_ref, bucket_ref, rel_ref, sink_ref,
                       o_ref, ko_ref, vo_ref, bias_s, *, heads, kv_heads):
    @pl.when(pl.program_id(0) == 0)
    def _():
        _build_bias(bias_s, bucket_ref, rel_ref, heads)

    bb, rows, qw = q_ref.shape
    window = kc_ref.shape[1]
    nk = bias_s.shape[2]
    qpad = jnp.zeros((SUBLANE - rows, qw), F32)
    kpad = jnp.zeros((nk - window - rows, LANE), F32)
    for b in range(bb):
        k_new = jnp.concatenate([kn_ref[b], kpad], axis=0)
        v_new = jnp.concatenate([vn_ref[b], kpad], axis=0)
        k_all = jnp.concatenate([kc_ref[b], k_new], axis=0)
        v_all = jnp.concatenate([vc_ref[b], v_new], axis=0)
        q = jnp.concatenate([q_ref[b], qpad], axis=0)
        out = _swa_core(q, k_all, v_all, bias_s, sink_ref, None, kv_heads, heads // kv_heads)
        for j, o in enumerate(out):
            o_ref[b, :, j * LANE:(j + 1) * LANE] = o[:rows].astype(o_ref.dtype)
        ko_ref[b] = k_all[rows:rows + window]
        vo_ref[b] = v_all[rows:rows + window]


def _swa_sample(proj, cache_k, cache_v, rel_bias, sinks, batch, rows, window, heads, kv_heads, bb):
    qw = heads * SWA_HEAD_DIM
    kw = kv_heads * SWA_HEAD_DIM
    assert kw == LANE and batch % bb == 0 and rows <= SUBLANE
    kcol = qw // kw
    nk = -(-(window + rows) // LANE) * LANE
    bucket = jnp.full((SUBLANE, nk), -1, jnp.int32)
    bucket = bucket.at[:rows, :window + rows].set(_rel_bucket_table(rows, window + rows))
    proj3 = proj.reshape(batch, rows, proj.shape[-1])
    smem = pl.BlockSpec(memory_space=pltpu.SMEM)
    cache = pl.BlockSpec((bb, window, kw), lambda b: (b, 0, 0))
    return pl.pallas_call(
        functools.partial(_swa_sample_kernel, heads=heads, kv_heads=kv_heads),
        out_shape=(jax.ShapeDtypeStruct((batch, rows, qw), BF16),
                   jax.ShapeDtypeStruct((batch, window, kw), F32),
                   jax.ShapeDtypeStruct((batch, window, kw), F32)),
        grid=(batch // bb,),
        in_specs=[pl.BlockSpec((bb, rows, qw), lambda b: (b, 0, 0)),
                  pl.BlockSpec((bb, rows, kw), lambda b: (b, 0, kcol)),
                  pl.BlockSpec((bb, rows, kw), lambda b: (b, 0, kcol + 1)),
                  cache, cache,
                  pl.BlockSpec(bucket.shape, lambda b: (0, 0)),
                  smem, smem],
        out_specs=(pl.BlockSpec((bb, rows, qw), lambda b: (b, 0, 0)), cache, cache),
        scratch_shapes=[pltpu.VMEM((heads, SUBLANE, nk), F32)],
        compiler_params=_cparams(1),
        name="swa_sample",
    )(proj3, proj3, proj3, cache_k, cache_v, bucket, rel_bias, sinks)


def _residual_matmul_kernel(*refs):
    n_parts = (len(refs) - 2) // 2
    x_ref, o_ref = refs[0], refs[-1]
    acc = x_ref[...]
    for a_ref, w_ref in zip(refs[1:1 + n_parts], refs[1 + n_parts:1 + 2 * n_parts]):
        acc = acc + _dot(a_ref[...], w_ref[...])
    o_ref[...] = acc


def _residual_matmul(x, parts, w, tm, name):
    m, d = x.shape
    kp = parts[0].shape[1]
    tm = min(tm, m)
    assert m % tm == 0 and all(p.shape == (m, kp) for p in parts) and w.shape == (kp * len(parts), d)
    row = lambda width: pl.BlockSpec((tm, width), lambda i: (i, 0))
    w_blk = lambda j: pl.BlockSpec((kp, d), lambda i, j=j: (j, 0))
    return pl.pallas_call(
        _residual_matmul_kernel,
        out_shape=jax.ShapeDtypeStruct((m, d), F32),
        grid=(m // tm,),
        in_specs=[row(d)] + [row(kp)] * len(parts) + [w_blk(j) for j in range(len(parts))],
        out_specs=row(d),
        compiler_params=_cparams(1),
        name=name,
    )(x, *parts, *([w] * len(parts)))


def _mem_heads(q, mk, mv, heads):
    outs = []
    scale = MEM_HEAD_DIM ** -0.5
    for h in range(heads):
        hs = slice(h * MEM_HEAD_DIM, (h + 1) * MEM_HEAD_DIM)
        s = _dot_nt(q[:, hs].astype(BF16), mk[:, hs].astype(BF16)) * scale
        m = jnp.max(s, axis=-1, keepdims=True)
        p = jnp.exp(s - m)
        den = jnp.sum(p, axis=-1, keepdims=True)
        outs.append(_dot(p.astype(BF16), mv[:, hs].astype(BF16)) / den)
    return jnp.concatenate(outs, axis=-1)


def _mem_prompt_kernel(x_ref, g_ref, wq_ref, wo_ref, mk_ref, mv_ref, o_ref, *, heads):
    x = x_ref[...]
    h = _rms(x, g_ref[...]).astype(BF16)
    q = _dot(h, wq_ref[...])
    o = _mem_heads(q, mk_ref[0], mv_ref[0], heads)
    o_ref[...] = x + _dot(o.astype(BF16), wo_ref[...])


def _mem_prompt(x, gain, wq, wo, mk, mv, batch, seq, tm):
    d = x.shape[1]
    mw = wq.shape[1]
    n_mem = mk.shape[1]
    heads = mw // MEM_HEAD_DIM
    assert seq % tm == 0
    nt = seq // tm
    full = lambda shape: pl.BlockSpec(shape, lambda b, i: (0,) * len(shape))
    mem = pl.BlockSpec((1, n_mem, mw), lambda b, i: (b, 0, 0))
    return pl.pallas_call(
        functools.partial(_mem_prompt_kernel, heads=heads),
        out_shape=jax.ShapeDtypeStruct(x.shape, F32),
        grid=(batch, nt),
        in_specs=[pl.BlockSpec((tm, d), lambda b, i: (b * nt + i, 0)),
                  full((1, d)), full((d, mw)), full((mw, d)), mem, mem],
        out_specs=pl.BlockSpec((tm, d), lambda b, i: (b * nt + i, 0)),
        compiler_params=_cparams(2),
        name="mem_prompt",
    )(x, gain, wq, wo, mk, mv)


def _mem_sample_kernel(q_ref, mk_ref, mv_ref, o_ref, *, heads):
    bb, rows, mw = q_ref.shape
    qpad = jnp.zeros((SUBLANE - rows, mw), F32)
    for b in range(bb):
        q = jnp.concatenate([q_ref[b], qpad], axis=0)
        o = _mem_heads(q, mk_ref[b], mv_ref[b], heads)
        o_ref[b] = o[:rows].astype(o_ref.dtype)


def _mem_sample(x, gain, wq, wo, mk, mv, batch, rows, bb):
    mw = wq.shape[1]
    n_mem = mk.shape[1]
    heads = mw // MEM_HEAD_DIM
    assert batch % bb == 0 and rows <= SUBLANE
    q = _norm_matmul(x, gain, wq, 1024, mw, "mem_q_s").reshape(batch, rows, mw)
    mem = pl.BlockSpec((bb, n_mem, mw), lambda b: (b, 0, 0))
    tok = pl.BlockSpec((bb, rows, mw), lambda b: (b, 0, 0))
    o = pl.pallas_call(
        functools.partial(_mem_sample_kernel, heads=heads),
        out_shape=jax.ShapeDtypeStruct((batch, rows, mw), BF16),
        grid=(batch // bb,),
        in_specs=[tok, mem, mem],
        out_specs=tok,
        compiler_params=_cparams(1),
        name="mem_sample",
    )(q, mk, mv)
    return _residual_matmul(x, [o.reshape(batch * rows, mw)], wo, 512, "mem_out_s")


def _ffn_kernel(x_ref, g_ref, wg_ref, wu_ref, wd_ref, gf_ref, o_ref, h_ref, acc_ref):
    j = pl.program_id(1)
    tm = x_ref.shape[0]

    @pl.when(j == 0)
    def _():
        def init(r):
            x = x_ref[r, :]
            h_ref[r, :] = _rms(x, g_ref[...]).astype(BF16)
            acc_ref[r, :] = x
        _for_row_chunks(tm, init)

    h = h_ref[...]
    gate = _dot(h, wg_ref[...])
    up = _dot(h, wu_ref[...])
    act = (gate * jax.nn.sigmoid(gate) * up).astype(BF16)
    acc_ref[...] += _dot(act, wd_ref[...])

    @pl.when(j == pl.num_programs(1) - 1)
    def _():
        def final(r):
            o_ref[r, :] = _rms(acc_ref[r, :], gf_ref[...])
        _for_row_chunks(tm, final)


def _ffn(x, gain, wg, wu, wd, gain_final, tm, tf):
    m, d = x.shape
    f = wg.shape[1]
    tm = min(tm, m)
    assert m % tm == 0 and f % tf == 0
    vec = pl.BlockSpec((1, d), lambda i, j: (0, 0))
    return pl.pallas_call(
        _ffn_kernel,
        out_shape=jax.ShapeDtypeStruct((m, d), F32),
        grid=(m // tm, f // tf),
        in_specs=[pl.BlockSpec((tm, d), lambda i, j: (i, 0)), vec,
                  pl.BlockSpec((d, tf), lambda i, j: (0, j)),
                  pl.BlockSpec((d, tf), lambda i, j: (0, j)),
                  pl.BlockSpec((tf, d), lambda i, j: (j, 0)), vec],
        out_specs=pl.BlockSpec((tm, d), lambda i, j: (i, 0)),
        scratch_shapes=[pltpu.VMEM((tm, d), BF16), pltpu.VMEM((tm, d), F32)],
        compiler_params=_cparams(2),
        name="ffn",
    )(x, gain, wg, wu, wd, gain_final)


def kernel(x_prompt, x_sample, state_hgrn, cache_swa_k, cache_swa_v, cache_mem_k, cache_mem_v, mem_prompt, norm_mix_w, w_in, hgrn_lb_logits, hgrn_onorm_w, swa_sinks, rel_bias, w_out, norm_xattn_w, mem_norm_w, w_mq, w_mk, w_mv, w_mo, norm_ffn_w, w_gate, w_up, w_down, norm_final_w):
    batch, seq, d = x_prompt.shape
    dec_batch, dec_seq, _ = x_sample.shape
    depth, _, heads, dk, dv = state_hgrn.shape
    _, _, window, kv_heads, swa_dim = cache_swa_k.shape
    _, _, n_mem, mem_heads, mem_dim = cache_mem_k.shape
    swa_heads = swa_sinks.shape[1]
    assert depth == 1 and swa_dim == SWA_HEAD_DIM and mem_dim == MEM_HEAD_DIM
    assert window == REL_MAX_DIST
    hw = heads * dk
    qw = swa_heads * SWA_HEAD_DIM
    kw = kv_heads * SWA_HEAD_DIM
    mw = mem_heads * MEM_HEAD_DIM
    assert w_in.shape[2] == 4 * hw + qw + 2 * kw and heads * dv == hw

    row = lambda v: v.reshape(1, -1)
    bf = lambda w: w.astype(BF16)
    w_in_b, w_out_b = bf(w_in[0]), bf(w_out[0])
    w_h, w_s = w_in_b[:, :4 * hw], w_in_b[:, 4 * hw:]
    w_mq_b, w_mo_b = bf(w_mq[0]), bf(w_mo[0])
    w_mkv_b = jnp.concatenate([bf(w_mk[0]), bf(w_mv[0])], axis=1)
    w_g_b, w_u_b, w_d_b = bf(w_gate[0]), bf(w_up[0]), bf(w_down[0])
    g_mix, g_x, g_ffn = row(norm_mix_w[0]), row(norm_xattn_w[0]), row(norm_ffn_w[0])
    g_mem, g_fin, onw = row(mem_norm_w[0]), row(norm_final_w), row(hgrn_onorm_w[0])
    sinks = swa_sinks[0].reshape(1, swa_heads)

    def mixer_tail(x2, a_out, b_out):
        return _residual_matmul(x2, [a_out, b_out], w_out_b, 512, "out_proj")

    def ffn_tail(x2):
        return _ffn(x2, g_ffn, w_g_b, w_u_b, w_d_b, g_fin, 512, 512)

    xp = x_prompt.reshape(batch * seq, d)
    proj_h = _norm_matmul(xp, g_mix, w_h, 1024, 1024, "in_proj_hgrn")
    proj_s = _norm_matmul(xp, g_mix, w_s, 1024, qw + 2 * kw, "in_proj_swa")
    a_out, p_state = _hgrn_prompt(proj_h, hgrn_lb_logits, onw, batch, seq, heads, dk, dv)
    b_out = _swa_prompt(proj_s, rel_bias, sinks, batch, seq, window, swa_heads, kv_heads)
    x1 = mixer_tail(xp, a_out, b_out)
    mkv = _norm_matmul(mem_prompt.reshape(batch * n_mem, d), g_mem, w_mkv_b, 1024, 2 * mw, "mem_kv")
    p_mk = mkv[:, :mw].reshape(batch, n_mem, mw)
    p_mv = mkv[:, mw:].reshape(batch, n_mem, mw)
    x2 = _mem_prompt(x1, g_x, w_mq_b, w_mo_b, p_mk, p_mv, batch, seq, 512)
    y_prompt = ffn_tail(x2).reshape(batch, seq, d)
    proj_s3 = proj_s.reshape(batch, seq, qw + 2 * kw)
    kv_shape = (1, batch, window, kv_heads, SWA_HEAD_DIM)
    p_k = proj_s3[:, seq - window:, qw:qw + kw].reshape(kv_shape)
    p_v = proj_s3[:, seq - window:, qw + kw:].reshape(kv_shape)
    mem_shape = (1, batch, n_mem, mem_heads, MEM_HEAD_DIM)

    xs = x_sample.reshape(dec_batch * dec_seq, d)
    sproj_h = _norm_matmul(xs, g_mix, w_h, 1024, 1024, "in_proj_hgrn_s")
    sproj_s = _norm_matmul(xs, g_mix, w_s, 1024, qw + 2 * kw, "in_proj_swa_s")
    sa_out, s_state = _hgrn_sample(sproj_h, state_hgrn[0], hgrn_lb_logits, onw,
                                   dec_batch, dec_seq, heads, dk, dv, 8)
    sb_out, s_k, s_v = _swa_sample(sproj_s, cache_swa_k[0].reshape(dec_batch, window, kw),
                                   cache_swa_v[0].reshape(dec_batch, window, kw),
                                   rel_bias, sinks, dec_batch, dec_seq, window, swa_heads, kv_heads, 8)
    sx1 = mixer_tail(xs, sa_out.reshape(dec_batch * dec_seq, hw), sb_out.reshape(dec_batch * dec_seq, qw))
    sx2 = _mem_sample(sx1, g_x, w_mq_b, w_mo_b, cache_mem_k[0].reshape(dec_batch, n_mem, mw),
                      cache_mem_v[0].reshape(dec_batch, n_mem, mw), dec_batch, dec_seq, 8)
    y_sample = ffn_tail(sx2).reshape(dec_batch, dec_seq, d)
    skv_shape = (1, dec_batch, window, kv_heads, SWA_HEAD_DIM)

    return (y_prompt, y_sample,
            p_state[None], p_k, p_v, p_mk.reshape(mem_shape), p_mv.reshape(mem_shape),
            s_state[None], s_k.reshape(skv_shape), s_v.reshape(skv_shape))
```
